```python
import math
import jax, jax.numpy as jnp
from jax import lax
import numpy as np

D_MODEL = 1024
BATCH = 2
SEQ = 8192
DEPTH = 4

HEAD_DIM = 64
SWA_HEADS = 6
SWA_WIDTH = SWA_HEADS * HEAD_DIM
DILATED_BRANCHES = ((128, 1), (512, 4), (2048, 16))
SWA_BLOCK = 128
S5_CHANNELS = 256
S5_GROUP = 16
S5_GROUPS = S5_CHANNELS // S5_GROUP
S5_STATE = 64
S5_DT_MIN = 1e-3
S5_DT_MAX = 1e-1
MLA_HEADS = 6
MLA_NOPE = 64
MLA_ROPE = 32
MLA_QK = MLA_NOPE + MLA_ROPE
MLA_V = 64
MLA_WIDTH = MLA_HEADS * MLA_V
MLA_Q_LORA = 256
MLA_KV_LORA = 128
ROPE_THETA = 10000.0
ATTN_BLOCK = 128
D_MIX = SWA_WIDTH + S5_CHANNELS + MLA_WIDTH
IN_SPLITS = (SWA_WIDTH, SWA_WIDTH, SWA_WIDTH, S5_CHANNELS, MLA_Q_LORA, MLA_KV_LORA, MLA_ROPE)
IN_COLS = 3 * SWA_WIDTH + S5_CHANNELS + MLA_Q_LORA + MLA_KV_LORA + MLA_ROPE
T5_BUCKETS = 32
T5_MAX_DISTANCE = 2048
PEER_KEYS = 128
PEER_EXPERTS = PEER_KEYS * PEER_KEYS
PEER_HEADS = 8
PEER_TOPK = 16
PEER_DKEY = 256
PEER_BLOCK = 128
EPS = 1e-6

kernel_name = "hybrid_swa_s5_mla_peer_trunk"


def rms_norm(x, gain):
    xf = x.astype(jnp.float32)
    y = xf * lax.rsqrt(jnp.mean(xf * xf, axis=-1, keepdims=True) + EPS)
    return (y * gain.astype(jnp.float32)).astype(x.dtype)


def t5_bucket(dist):
    max_exact = T5_BUCKETS // 2
    d_f = jnp.maximum(dist, 1).astype(jnp.float32)
    large = max_exact + (jnp.log(d_f / max_exact) / math.log(T5_MAX_DISTANCE / max_exact)
                         * (T5_BUCKETS - max_exact)).astype(jnp.int32)
    large = jnp.minimum(large, T5_BUCKETS - 1)
    return jnp.where(dist < max_exact, dist, large)


def rope(x, cos, sin):
    half = x.shape[-1] // 2
    x1, x2 = x[..., :half], x[..., half:]
    return jnp.concatenate([x1 * cos - x2 * sin, x1 * sin + x2 * cos], axis=-1)


def dilated_branch(q, k, v, bias_j, dil):
    B, H, S, hd = q.shape
    n_back = bias_j.shape[-1] - 1
    L = S // dil
    nb = -(-L // SWA_BLOCK)
    Lp = nb * SWA_BLOCK

    def to_blocks(t):
        t = t.reshape(B, H, L, dil, hd).transpose(0, 1, 3, 2, 4)
        t = jnp.pad(t, ((0, 0), (0, 0), (0, 0), (0, Lp - L), (0, 0)))
        return t.reshape(B, H, dil, nb, SWA_BLOCK, hd)

    def with_prev(t):
        prev = jnp.pad(t[:, :, :, :-1], ((0, 0), (0, 0), (0, 0), (1, 0), (0, 0), (0, 0)))
        return jnp.concatenate([prev, t], axis=4)

    qb = to_blocks(q)
    kc = with_prev(to_blocks(k))
    vc = with_prev(to_blocks(v)).astype(jnp.float32)
    logits = jnp.einsum('bhrnqd,bhrnkd->bhrnqk', qb, kc).astype(jnp.float32)
    qi = jnp.arange(SWA_BLOCK)[:, None]
    kj = jnp.arange(2 * SWA_BLOCK)[None, :]
    delta = SWA_BLOCK + qi - kj
    band = (delta >= 0) & (delta <= n_back)
    has_prev = (jnp.arange(nb)[:, None, None] > 0) | (kj >= SWA_BLOCK)[None]
    valid = band[None] & has_prev
    bias = bias_j.astype(jnp.float32)[:, jnp.clip(delta, 0, n_back)]
    logits = jnp.where(valid, logits + bias[:, None, None], -jnp.inf)
    mx = jnp.max(logits, axis=-1)
    p = jnp.exp(logits - mx[..., None])
    den = jnp.sum(p, axis=-1)
    num = jnp.einsum('bhrnqk,bhrnkd->bhrnqd', p, vc)

    def from_blocks(t):
        t = t.reshape((B, H, dil, Lp) + t.shape[5:])[:, :, :, :L]
        t = jnp.swapaxes(t, 2, 3)
        return t.reshape((B, H, S) + t.shape[4:])

    return from_blocks(num), from_blocks(den), from_blocks(mx)


def dilated_attention(q, k, v, branch_biases):
    nums, dens, mxs = [], [], []
    for (window, dil), bias_j in zip(DILATED_BRANCHES, branch_biases):
        n_i, d_i, m_i = dilated_branch(q, k, v, bias_j, dil)
        nums.append(n_i)
        dens.append(d_i)
        mxs.append(m_i)
    mx = jnp.stack(mxs)
    w = jnp.exp(mx - jnp.max(mx, axis=0, keepdims=True))
    num = jnp.sum(w[..., None] * jnp.stack(nums), axis=0)
    den = jnp.sum(w * jnp.stack(dens), axis=0)
    return num / den[..., None]


def mla_attention(q, k, v):
    B, H, S, _ = q.shape
    nq = S // ATTN_BLOCK
    qb = q.reshape(B, H, nq, ATTN_BLOCK, MLA_QK).transpose(2, 0, 1, 3, 4)
    k_pos = jnp.arange(S)
    vf = v.astype(jnp.float32)

    def block(args):
        qblk, n = args
        logits = jnp.einsum('bhqd,bhkd->bhqk', qblk, k).astype(jnp.float32)
        q_pos = n * ATTN_BLOCK + jnp.arange(ATTN_BLOCK)
        logits = jnp.where(k_pos[None, :] <= q_pos[:, None], logits, -jnp.inf)
        p = jax.nn.softmax(logits, axis=-1)
        return jnp.einsum('bhqk,bhkd->bhqd', p, vf)

    out = lax.map(block, (qb, jnp.arange(nq)))
    return out.transpose(1, 2, 0, 3, 4).reshape(B, H, S, MLA_V)


def s5_combine(e1, e2):
    a1r, a1i, b1r, b1i = e1
    a2r, a2i, b2r, b2i = e2
    return (a2r * a1r - a2i * a1i,
            a2r * a1i + a2i * a1r,
            a2r * b1r - a2i * b1i + b2r,
            a2r * b1i + a2i * b1r + b2i)


def s5_mixer(u, a_re, a_im, b_re, b_im, c_re, c_im, d_skip, log_dt, w_glu, b_glu):
    B, S, _ = u.shape
    f32 = jnp.float32
    uf = u.astype(f32)
    ug = uf.reshape(B, S, S5_GROUPS, S5_GROUP)
    a_re, a_im = a_re.astype(f32), a_im.astype(f32)
    b_re, b_im = b_re.astype(f32), b_im.astype(f32)
    dt = jnp.exp(log_dt.astype(f32))[:, None]
    mag = jnp.exp(a_re * dt)
    lam_re, lam_im = mag * jnp.cos(a_im * dt), mag * jnp.sin(a_im * dt)
    den = a_re * a_re + a_im * a_im
    nr, ni = lam_re - 1.0, lam_im
    coef_re = (nr * a_re + ni * a_im) / den
    coef_im = (ni * a_re - nr * a_im) / den
    bb_re = coef_re[..., None] * b_re - coef_im[..., None] * b_im
    bb_im = coef_re[..., None] * b_im + coef_im[..., None] * b_re
    bu_re = jnp.einsum('gpc,bsgc->bsgp', bb_re, ug)
    bu_im = jnp.einsum('gpc,bsgc->bsgp', bb_im, ug)
    shape = bu_re.shape
    elems = (jnp.broadcast_to(lam_re, shape), jnp.broadcast_to(lam_im, shape), bu_re, bu_im)
    _, _, xr, xi = lax.associative_scan(s5_combine, elems, axis=1)
    y = (jnp.einsum('gcp,bsgp->bsgc', c_re.astype(f32), xr)
         - jnp.einsum('gcp,bsgp->bsgc', c_im.astype(f32), xi))
    y = y.reshape(B, S, S5_CHANNELS) + d_skip.astype(f32) * uf
    g = jax.nn.gelu(y)
    return g * jax.nn.sigmoid(g @ w_glu.astype(f32) + b_glu.astype(f32))


def token_mixer(h, w_in, w_out, mix_gain, swa_q_gain, swa_k_gain, branch_biases,
                mla_q_lora_gain, mla_kv_lora_gain, w_uq, w_ukv, mla_q_gain, mla_k_gain,
                cos, sin, s5_params):
    B, S, _ = h.shape
    proj = h @ w_in
    cuts = [sum(IN_SPLITS[:i + 1]) for i in range(len(IN_SPLITS) - 1)]
    qa, ka, va, u, cq, ckv, kr = jnp.split(proj, cuts, axis=-1)

    def heads(t, n):
        return t.reshape(B, S, n, -1).transpose(0, 2, 1, 3)
    qa = rms_norm(heads(qa, SWA_HEADS), swa_q_gain) * (HEAD_DIM ** -0.5)
    ka = rms_norm(heads(ka, SWA_HEADS), swa_k_gain)
    ya = dilated_attention(qa, ka, heads(va, SWA_HEADS), branch_biases)
    ya = ya.transpose(0, 2, 1, 3).reshape(B, S, SWA_WIDTH)

    yb = s5_mixer(u, *s5_params)

    q = (rms_norm(cq, mla_q_lora_gain) @ w_uq).reshape(B, S, MLA_HEADS, MLA_QK)
    kv = (rms_norm(ckv, mla_kv_lora_gain) @ w_ukv).reshape(B, S, MLA_HEADS, MLA_NOPE + MLA_V)
    k_nope, vc = kv[..., :MLA_NOPE], kv[..., MLA_NOPE:]
    q_nope = rms_norm(q[..., :MLA_NOPE], mla_q_gain[:MLA_NOPE])
    q_rope = rope(rms_norm(q[..., MLA_NOPE:], mla_q_gain[MLA_NOPE:]), cos, sin)
    k_nope = rms_norm(k_nope, mla_k_gain[:MLA_NOPE])
    k_rope = rope(rms_norm(kr[:, :, None, :], mla_k_gain[MLA_NOPE:]), cos, sin)
    qc = jnp.concatenate([q_nope, q_rope], axis=-1) * (MLA_QK ** -0.5)
    kc = jnp.concatenate([k_nope, jnp.broadcast_to(k_rope, (B, S, MLA_HEADS, MLA_ROPE))], axis=-1)
    yc = mla_attention(qc.transpose(0, 2, 1, 3), kc.transpose(0, 2, 1, 3), vc.transpose(0, 2, 1, 3))
    yc = yc.transpose(0, 2, 1, 3).reshape(B, S, MLA_WIDTH)

    y = jnp.concatenate([
        rms_norm(ya, mix_gain[:SWA_WIDTH]),
        rms_norm(yb, mix_gain[SWA_WIDTH:SWA_WIDTH + S5_CHANNELS]),
        rms_norm(yc, mix_gain[SWA_WIDTH + S5_CHANNELS:]),
    ], axis=-1)
    return y @ w_out


def peer_ffn(h, wq, subkeys, u_tab, v_tab):
    B, S, D = h.shape
    T = B * S
    hf = h.reshape(T, D)
    q = (hf @ wq).reshape(T, PEER_HEADS, 2, PEER_DKEY // 2)
    scores = jnp.einsum('thjd,hjkd->thjk', q, subkeys).astype(jnp.float32)
    top_s, top_i = lax.top_k(scores, PEER_TOPK)
    cand = top_s[:, :, 0, :, None] + top_s[:, :, 1, None, :]
    best_s, best_c = lax.top_k(cand.reshape(T, PEER_HEADS, PEER_TOPK * PEER_TOPK), PEER_TOPK)
    i1 = jnp.take_along_axis(top_i[:, :, 0], best_c // PEER_TOPK, axis=-1)
    i2 = jnp.take_along_axis(top_i[:, :, 1], best_c % PEER_TOPK, axis=-1)
    nblk = T // PEER_BLOCK
    experts = (i1 * PEER_KEYS + i2).reshape(nblk, PEER_BLOCK, PEER_HEADS * PEER_TOPK)
    gates = jax.nn.softmax(best_s, axis=-1).reshape(nblk, PEER_BLOCK, PEER_HEADS * PEER_TOPK)
    xb = hf.reshape(nblk, PEER_BLOCK, D)

    def block(args):
        xt, et, gt = args
        act = jax.nn.gelu(jnp.einsum('td,tkd->tk', xt, u_tab[et]).astype(jnp.float32))
        return jnp.einsum('tk,tkd->td', gt * act, v_tab[et].astype(jnp.float32))

    out = lax.map(block, (xb, experts, gates))
    return out.reshape(B, S, D)


def setup_inputs(seed: int = 0) -> dict:
    key = jax.random.key(seed)
    ks = jax.random.split(key, 40)
    f32 = jnp.float32

    def nrm(k, shape, scale):
        return jax.random.normal(k, shape, f32) * scale

    def gain(k, shape):
        return 1.0 + 0.02 * jax.random.normal(k, shape, f32)

    L, D = DEPTH, D_MODEL
    return {
        "x": nrm(ks[0], (BATCH, SEQ, D), 1.0),
        "c": nrm(ks[1], (BATCH, D), 1.0),
        "w_in": nrm(ks[2], (L, D, IN_COLS), D ** -0.5),
        "w_out": nrm(ks[3], (L, D_MIX, D), D_MIX ** -0.5),
        "mix_gain": gain(ks[4], (L, D_MIX)),
        "norm1_gain": gain(ks[5], (L, D)),
        "norm2_gain": gain(ks[6], (L, D)),
        "w_ada": nrm(ks[7], (L, D, 6 * D), 0.5 * D ** -0.5),
        "b_ada": nrm(ks[8], (L, 6 * D), 0.02),
        "rel_bias": nrm(ks[9], (SWA_HEADS, T5_BUCKETS), 0.5),
        "swa_q_gain": gain(ks[10], (L, HEAD_DIM)),
        "swa_k_gain": gain(ks[11], (L, HEAD_DIM)),
        "mla_q_lora_gain": gain(ks[12], (L, MLA_Q_LORA)),
        "mla_kv_lora_gain": gain(ks[13], (L, MLA_KV_LORA)),
        "w_uq": nrm(ks[14], (L, MLA_Q_LORA, MLA_HEADS * MLA_QK), MLA_Q_LORA ** -0.5),
        "w_ukv": nrm(ks[15], (L, MLA_KV_LORA, MLA_HEADS * (MLA_NOPE + MLA_V)), MLA_KV_LORA ** -0.5),
        "mla_q_gain": gain(ks[16], (L, MLA_QK)),
        "mla_k_gain": gain(ks[17], (L, MLA_QK)),
        "s5_a_re": -0.5 + nrm(ks[18], (L, S5_GROUPS, S5_STATE), 0.01),
        "s5_a_im": math.pi * jnp.arange(S5_STATE, dtype=f32) + nrm(ks[19], (L, S5_GROUPS, S5_STATE), 0.01),
        "s5_b_re": nrm(ks[20], (L, S5_GROUPS, S5_STATE, S5_GROUP), (2 * S5_GROUP) ** -0.5),
        "s5_b_im": nrm(ks[21], (L, S5_GROUPS, S5_STATE, S5_GROUP), (2 * S5_GROUP) ** -0.5),
        "s5_c_re": nrm(ks[22], (L, S5_GROUPS, S5_GROUP, S5_STATE), (2 * S5_STATE) ** -0.5),
        "s5_c_im": nrm(ks[23], (L, S5_GROUPS, S5_GROUP, S5_STATE), (2 * S5_STATE) ** -0.5),
        "s5_d": nrm(ks[24], (L, S5_CHANNELS), 1.0),
        "s5_log_dt": jax.random.uniform(ks[25], (L, S5_GROUPS), f32, math.log(S5_DT_MIN), math.log(S5_DT_MAX)),
        "w_glu": nrm(ks[26], (L, S5_CHANNELS, S5_CHANNELS), S5_CHANNELS ** -0.5),
        "b_glu": nrm(ks[27], (L, S5_CHANNELS), 0.02),
        "peer_wq": nrm(ks[28], (L, D, PEER_HEADS * PEER_DKEY), D ** -0.5),
        "peer_subkeys": nrm(ks[29], (L, PEER_HEADS, 2, PEER_KEYS, PEER_DKEY // 2), (PEER_DKEY // 2) ** -0.5),
        "peer_u": nrm(ks[30], (L, PEER_EXPERTS, D), D ** -0.5),
        "peer_v": nrm(ks[31], (L, PEER_EXPERTS, D), 0.5),
    }


def reference(x, c, w_in, w_out, mix_gain, norm1_gain, norm2_gain, w_ada, b_ada, rel_bias,
              swa_q_gain, swa_k_gain, mla_q_lora_gain, mla_kv_lora_gain, w_uq, w_ukv,
              mla_q_gain, mla_k_gain, s5_a_re, s5_a_im, s5_b_re, s5_b_im, s5_c_re, s5_c_im,
              s5_d, s5_log_dt, w_glu, b_glu, peer_wq, peer_subkeys, peer_u, peer_v):
    B, S, _ = x.shape
    f32 = jnp.float32
    pos = jnp.arange(S, dtype=f32)
    inv_freq = ROPE_THETA ** (-jnp.arange(0, MLA_ROPE, 2, dtype=f32) / MLA_ROPE)
    ang = pos[:, None] * inv_freq[None, :]
    cos, sin = jnp.cos(ang)[:, None, :], jnp.sin(ang)[:, None, :]
    branch_biases = [rel_bias[:, t5_bucket(dil * jnp.arange(window // dil + 1))]
                     for window, dil in DILATED_BRANCHES]
    c_act = jax.nn.silu(c)
    for l in range(DEPTH):
        mod = c_act @ w_ada[l] + b_ada[l]
        sh1, sc1, g1, sh2, sc2, g2 = jnp.split(mod, 6, axis=-1)
        h = rms_norm(x, norm1_gain[l]) * (1.0 + sc1[:, None]) + sh1[:, None]
        s5_params = (s5_a_re[l], s5_a_im[l], s5_b_re[l], s5_b_im[l], s5_c_re[l], s5_c_im[l],
                     s5_d[l], s5_log_dt[l], w_glu[l], b_glu[l])
        y = token_mixer(h, w_in[l], w_out[l], mix_gain[l], swa_q_gain[l], swa_k_gain[l], branch_biases,
                        mla_q_lora_gain[l], mla_kv_lora_gain[l], w_uq[l], w_ukv[l],
                        mla_q_gain[l], mla_k_gain[l], cos, sin, s5_params)
        x = x + g1[:, None] * y
        h = rms_norm(x, norm2_gain[l]) * (1.0 + sc2[:, None]) + sh2[:, None]
        x = x + g2[:, None] * peer_ffn(h, peer_wq[l], peer_subkeys[l], peer_u[l], peer_v[l])
    return x
```

```python
import functools
import math

import numpy as np
import jax
import jax.numpy as jnp
from jax import lax
from jax.experimental import pallas as pl
from jax.experimental.pallas import tpu as pltpu

F32 = jnp.float32
BF16 = jnp.bfloat16

D_MODEL = 1024
HEAD_DIM = 64
SWA_HEADS = 6
SWA_WIDTH = SWA_HEADS * HEAD_DIM
DILATED_BRANCHES = ((128, 1), (512, 4), (2048, 16))
SWA_BLOCK = 128
S5_CHANNELS = 256
S5_GROUP = 16
S5_GROUPS = 16
S5_STATE = 64
S5_LANES = S5_GROUPS * S5_STATE
MLA_HEADS = 6
MLA_NOPE = 64
MLA_ROPE = 32
MLA_QK = 96
MLA_V = 64
MLA_WIDTH = MLA_HEADS * MLA_V
MLA_Q_LORA = 256
MLA_KV_LORA = 128
MLA_PAD = 128
ROPE_THETA = 10000.0
T5_BUCKETS = 32
T5_MAX_DISTANCE = 2048
PEER_KEYS = 128
PEER_EXPERTS = PEER_KEYS * PEER_KEYS
PEER_HEADS = 8
PEER_TOPK = 16
PEER_DKEY = 256
EPS = 1e-6
NEG = -1e30

LANE = 128
SUBLANE = 8

TM_PROJ = 512
S5_CHUNK = 256
S5_INNER = SUBLANE
MLA_TQ = 512
MLA_TK = 512
PEER_TM = 512
PEER_EC = 1024
VMEM_LIMIT = 56 * 1024 * 1024


def _cp(n_axes, vmem=VMEM_LIMIT):
    return pltpu.CompilerParams(dimension_semantics=("arbitrary",) * n_axes, vmem_limit_bytes=vmem)


def _dot(a, b):
    return jnp.dot(a, b, preferred_element_type=F32)


def _dot_nt(a, b):
    return lax.dot_general(a, b, (((1,), (1,)), ((), ())), preferred_element_type=F32)


def _gelu(x):
    return 0.5 * x * (1.0 + jnp.tanh(math.sqrt(2.0 / math.pi) * (x + 0.044715 * (x * x * x))))


def _sigmoid(x):
    return 1.0 / (1.0 + jnp.exp(-x))


def _full(shape):
    nd = len(shape)
    return pl.BlockSpec(shape, lambda *_: (0,) * nd)


def _mod_kernel(c_ref, w_ref, b_ref, o_ref):
    c = c_ref[...]
    ca = c * _sigmoid(c)
    o_ref[0] = _dot(ca.astype(BF16), w_ref[0].astype(BF16)) + b_ref[0]


def _mod_call(c, w_ada, b_ada):
    L, D, N = w_ada.shape
    B = c.shape[0]
    tn = 1536
    return pl.pallas_call(
        _mod_kernel,
        out_shape=jax.ShapeDtypeStruct((L, B, N), F32),
        grid=(L, N // tn),
        in_specs=[
            pl.BlockSpec((B, D), lambda l, j: (0, 0)),
            pl.BlockSpec((1, D, tn), lambda l, j: (l, 0, j)),
            pl.BlockSpec((1, 1, tn), lambda l, j: (l, 0, j)),
        ],
        out_specs=pl.BlockSpec((1, B, tn), lambda l, j: (l, 0, j)),
        compiler_params=_cp(2),
        name="adaln_mod",
    )(c, w_ada, b_ada.reshape(L, 1, N))


def _inproj_kernel(x_ref, mod_ref, n1g_ref, wa_ref, wu_ref, wc_ref, wuqa_ref, wuqb_ref, wkn_ref, wv_ref,
                   g384_ref, gq_ref, gk_ref, swag_ref, lorag_ref, qg_ref, kg_ref, cos_ref, sin_ref,
                   qa_o, ka_o, va_o, u_o, q_o, k_o, v_o, *, tiles_per_batch):
    x = x_ref[...]
    b = pl.program_id(0) // tiles_per_batch
    sh = mod_ref[pl.ds(b, 1), 0:D_MODEL]
    sc = mod_ref[pl.ds(b, 1), D_MODEL:2 * D_MODEL]
    ms = jnp.mean(x * x, axis=-1, keepdims=True)
    h = (x * lax.rsqrt(ms + EPS) * n1g_ref[...]) * (1.0 + sc) + sh
    hb = h.astype(BF16)

    pa = _dot(hb, wa_ref[...])
    qa = pa[:, 0:SWA_WIDTH]
    ka = pa[:, SWA_WIDTH:2 * SWA_WIDTH]
    g384 = g384_ref[...]
    qa_ms = _dot((qa * qa).astype(BF16), g384)
    ka_ms = _dot((ka * ka).astype(BF16), g384)
    qa_o[...] = (qa * lax.rsqrt(qa_ms + EPS) * swag_ref[0:1, :]).astype(BF16)
    ka_o[...] = (ka * lax.rsqrt(ka_ms + EPS) * swag_ref[1:2, :]).astype(BF16)
    va_o[...] = pa[:, 2 * SWA_WIDTH:3 * SWA_WIDTH].astype(BF16)

    u_o[...] = _dot(hb, wu_ref[...])

    pc = _dot(hb, wc_ref[...])
    cq = pc[:, 0:MLA_Q_LORA]
    ckv = pc[:, MLA_Q_LORA:MLA_Q_LORA + MLA_KV_LORA]
    kra = pc[:, 384:512]
    krb = pc[:, 512:640]
    cqn = cq * lax.rsqrt(jnp.mean(cq * cq, axis=-1, keepdims=True) + EPS) * lorag_ref[0:1, :]
    ckvn = (ckv * lax.rsqrt(jnp.mean(ckv * ckv, axis=-1, keepdims=True) + EPS)
            * lorag_ref[1:2, 0:MLA_KV_LORA])
    cqb = cqn.astype(BF16)
    ckvb = ckvn.astype(BF16)
    cos = cos_ref[...]
    sin = sin_ref[...]
    cos6 = jnp.concatenate([cos] * MLA_HEADS, axis=1)
    sin6 = jnp.concatenate([sin] * MLA_HEADS, axis=1)

    q_a = _dot(cqb, wuqa_ref[...])
    q_b = _dot(cqb, wuqb_ref[...])
    q_ms = _dot((q_a * q_a).astype(BF16), gq_ref[...])
    q = lax.rsqrt(q_ms + EPS) * (q_a * qg_ref[0:1, :] * cos6 + q_b * qg_ref[1:2, :] * sin6)
    q_o[...] = q.astype(BF16)

    k_n = _dot(ckvb, wkn_ref[...])
    k_ms = _dot((k_n * k_n).astype(BF16), gk_ref[...])
    kn = k_n * lax.rsqrt(k_ms + EPS) * kg_ref[0:1, :]
    kr_ms = jnp.sum(kra * kra, axis=-1, keepdims=True) * (1.0 / MLA_ROPE)
    kr = lax.rsqrt(kr_ms + EPS) * (kra * kg_ref[1:2, 0:MLA_PAD] * cos + krb * kg_ref[2:3, 0:MLA_PAD] * sin)
    k_o[...] = (kn + jnp.concatenate([kr] * MLA_HEADS, axis=1)).astype(BF16)
    v_o[...] = _dot(ckvb, wv_ref[...]).astype(BF16)


def _inproj_call(x2d, mod_l, lw, cos_t, sin_t, S):
    T = x2d.shape[0]
    tm = TM_PROJ
    tpb = S // tm
    row = lambda i: (i, 0)
    ins = [x2d, mod_l, lw["n1g"], lw["wa"], lw["wu"], lw["wc"], lw["wuqa"], lw["wuqb"], lw["wkn"], lw["wv"],
           lw["g384"], lw["gq"], lw["gk"], lw["swag"], lw["lorag"], lw["qg"], lw["kg"], cos_t, sin_t]
    in_specs = [pl.BlockSpec((tm, D_MODEL), row)] + [_full(a.shape) for a in ins[1:17]]
    in_specs += [pl.BlockSpec((tm, MLA_PAD), lambda i: (i % tpb, 0))] * 2
    widths = [(SWA_WIDTH, BF16), (SWA_WIDTH, BF16), (SWA_WIDTH, BF16), (S5_CHANNELS, F32),
              (MLA_HEADS * MLA_PAD, BF16), (MLA_HEADS * MLA_PAD, BF16), (MLA_WIDTH, BF16)]
    return pl.pallas_call(
        functools.partial(_inproj_kernel, tiles_per_batch=tpb),
        out_shape=[jax.ShapeDtypeStruct((T, w), dt) for w, dt in widths],
        grid=(T // tm,),
        in_specs=in_specs,
        out_specs=[pl.BlockSpec((tm, w), row) for w, _ in widths],
        compiler_params=_cp(1),
        name="in_proj",
    )(*ins)


def _swa_kernel(rel_ref, idx_ref, q_ref, kc_ref, kp_ref, vc_ref, vp_ref, o_ref, lse_ref, bias_sc):
    n = pl.program_id(2)
    first = jnp.logical_and(jnp.logical_and(pl.program_id(0) == 0, pl.program_id(1) == 0), n == 0)

    @pl.when(first)
    def _():
        idx = idx_ref[...]
        for h in range(SWA_HEADS):
            def body(bk, acc):
                return jnp.where(idx == bk, rel_ref[h, bk], acc)
            bias_sc[h] = lax.fori_loop(0, T5_BUCKETS, body, jnp.full(idx.shape, NEG, F32))

    has_prev = n > 0
    for h in range(SWA_HEADS):
        sl = slice(h * HEAD_DIM, (h + 1) * HEAD_DIM)
        q = q_ref[0, :, sl]
        bias = bias_sc[h]
        s_p = _dot_nt(q, kp_ref[0, :, sl]) + jnp.where(has_prev, bias[:, 0:SWA_BLOCK], NEG)
        s_c = _dot_nt(q, kc_ref[0, :, sl]) + bias[:, SWA_BLOCK:2 * SWA_BLOCK]
        m = jnp.maximum(jnp.max(s_p, axis=-1, keepdims=True), jnp.max(s_c, axis=-1, keepdims=True))
        p_p = jnp.exp(s_p - m)
        p_c = jnp.exp(s_c - m)
        den = jnp.sum(p_p, axis=-1, keepdims=True) + jnp.sum(p_c, axis=-1, keepdims=True)
        num = _dot(p_p.astype(BF16), vp_ref[0, :, sl]) + _dot(p_c.astype(BF16), vc_ref[0, :, sl])
        o_ref[0, :, sl] = num / den
        lse_ref[0, :, sl] = jnp.broadcast_to(m + jnp.log(den), (SWA_BLOCK, HEAD_DIM))


def _swa_branch(qa, ka, va, rel_bias, idx_map, B, S, dil):
    L = S // dil
    nb = L // SWA_BLOCK
    W = SWA_WIDTH
    view = lambda t: t.reshape(B, L, dil * W)
    cur = pl.BlockSpec((1, SWA_BLOCK, W), lambda b, r, n: (b, n, r))
    prev = pl.BlockSpec((1, SWA_BLOCK, W), lambda b, r, n: (b, jnp.maximum(n - 1, 0), r))
    o, lse = pl.pallas_call(
        _swa_kernel,
        out_shape=[jax.ShapeDtypeStruct((B, L, dil * W), F32)] * 2,
        grid=(B, dil, nb),
        in_specs=[pl.BlockSpec(memory_space=pltpu.SMEM), _full(idx_map.shape), cur, cur, prev, cur, prev],
        out_specs=[cur, cur],
        scratch_shapes=[pltpu.VMEM((SWA_HEADS, SWA_BLOCK, 2 * SWA_BLOCK), F32)],
        compiler_params=_cp(3),
        name=f"swa_dil{dil}",
    )(rel_bias, idx_map, view(qa), view(ka), view(ka), view(va), view(va))
    return o.reshape(B * S, W), lse.reshape(B * S, W)


def _s5_kernel(u_ref, bbre_ref, bbim_ref, tri_ref, pn_ref, pp_ref, pc_ref, cre_ref, cim_ref,
               dsk_ref, wglu_ref, bglu_ref, mg_ref, o_ref, xre_sc, xim_sc, car_sc):
    @pl.when(pl.program_id(1) == 0)
    def _():
        car_sc[...] = jnp.zeros_like(car_sc)

    u = u_ref[...]
    ub = u.astype(BF16)
    bu_re = _dot(ub, bbre_ref[...])
    bu_im = _dot(ub, bbim_ref[...])
    pn_re, pn_im = pn_ref[0], pn_ref[1]
    z_re = (pn_re * bu_re - pn_im * bu_im).astype(BF16)
    z_im = (pn_re * bu_im + pn_im * bu_re).astype(BF16)
    tri = tri_ref[...]
    c_re = _dot(tri, z_re)
    c_im = _dot(tri, z_im)
    pp_re, pp_im = pp_ref[0], pp_ref[1]
    xre_sc[...] = pp_re * c_re - pp_im * c_im
    xim_sc[...] = pp_re * c_im + pp_im * c_re

    pc_re, pc_im = pc_ref[0], pc_ref[1]

    def body(j, carry):
        cr, ci = carry
        r0 = pl.multiple_of(j * S5_INNER, S5_INNER)
        nr = xre_sc[pl.ds(r0, S5_INNER), :] + (pc_re * cr - pc_im * ci)
        ni = xim_sc[pl.ds(r0, S5_INNER), :] + (pc_re * ci + pc_im * cr)
        xre_sc[pl.ds(r0, S5_INNER), :] = nr
        xim_sc[pl.ds(r0, S5_INNER), :] = ni
        return nr[S5_INNER - 1:S5_INNER, :], ni[S5_INNER - 1:S5_INNER, :]

    cr, ci = lax.fori_loop(0, S5_CHUNK // S5_INNER, body, (car_sc[0:1, :], car_sc[1:2, :]))
    car_sc[0:1, :] = cr
    car_sc[1:2, :] = ci

    y = _dot(xre_sc[...].astype(BF16), cre_ref[...]) - _dot(xim_sc[...].astype(BF16), cim_ref[...])
    y = y + dsk_ref[...] * u
    g = _gelu(y)
    yb = g * _sigmoid(_dot(g.astype(BF16), wglu_ref[...]) + bglu_ref[...])
    ms = jnp.mean(yb * yb, axis=-1, keepdims=True)
    o_ref[...] = (yb * lax.rsqrt(ms + EPS) * mg_ref[...]).astype(BF16)


def _s5_call(u, lw, B, S):
    T = u.shape[0]
    nc = S // S5_CHUNK
    ins = [u, lw["bbre"], lw["bbim"], lw["tri"], lw["pneg"], lw["ppos"], lw["pcar"], lw["cre"], lw["cim"],
           lw["dsk"], lw["wglu"], lw["bglu"], lw["mgb"]]
    row = lambda b, c: (b * nc + c, 0)
    return pl.pallas_call(
        _s5_kernel,
        out_shape=jax.ShapeDtypeStruct((T, S5_CHANNELS), BF16),
        grid=(B, nc),
        in_specs=[pl.BlockSpec((S5_CHUNK, S5_CHANNELS), row)] + [_full(a.shape) for a in ins[1:]],
        out_specs=pl.BlockSpec((S5_CHUNK, S5_CHANNELS), row),
        scratch_shapes=[pltpu.VMEM((S5_CHUNK, S5_LANES), F32), pltpu.VMEM((S5_CHUNK, S5_LANES), F32),
                        pltpu.VMEM((2, S5_LANES), F32)],
        compiler_params=_cp(2),
        name="s5_scan",
    )(*ins)


def _mla_kernel(qi_ref, kj_ref, q_ref, k_ref, v_ref, o_ref, m_sc, l_sc, acc_sc):
    p = pl.program_id(2)
    i = qi_ref[p]
    j = kj_ref[p]

    @pl.when(j == 0)
    def _():
        m_sc[...] = jnp.full_like(m_sc, NEG)
        l_sc[...] = jnp.zeros_like(l_sc)
        acc_sc[...] = jnp.zeros_like(acc_sc)

    row = lax.broadcasted_iota(jnp.int32, (MLA_TQ, MLA_TK), 0)
    col = lax.broadcasted_iota(jnp.int32, (MLA_TQ, MLA_TK), 1)
    keep = jnp.logical_or(j < i, row >= col)
    v = v_ref[...]
    for hh in range(2):
        sl = slice(hh * MLA_PAD, (hh + 1) * MLA_PAD)
        s = jnp.where(keep, _dot_nt(q_ref[:, sl], k_ref[:, sl]), NEG)
        m_prev = m_sc[hh]
        m_new = jnp.maximum(m_prev, jnp.max(s, axis=-1, keepdims=True))
        alpha = jnp.exp(m_prev - m_new)
        pr = jnp.exp(s - m_new[:, 0:1])
        l_sc[hh] = alpha * l_sc[hh] + jnp.sum(pr, axis=-1, keepdims=True)
        acc_sc[hh] = alpha * acc_sc[hh] + _dot(pr.astype(BF16), v)
        m_sc[hh] = m_new

    @pl.when(j == i)
    def _():
        lane = lax.broadcasted_iota(jnp.int32, (MLA_TQ, LANE), 1)
        o_ref[...] = jnp.where(lane < MLA_V, acc_sc[0] / l_sc[0], acc_sc[1] / l_sc[1])


def _mla_call(q, k, v, B, S):
    T = q.shape[0]
    nq = S // MLA_TQ
    pairs = [(i, j) for i in range(nq) for j in range(i + 1)]
    qi = jnp.asarray(np.array([p[0] for p in pairs], np.int32))
    kj = jnp.asarray(np.array([p[1] for p in pairs], np.int32))
    grid_spec = pltpu.PrefetchScalarGridSpec(
        num_scalar_prefetch=2,
        grid=(B, MLA_HEADS // 2, len(pairs)),
        in_specs=[
            pl.BlockSpec((MLA_TQ, 2 * MLA_PAD), lambda b, hp, p, qi, kj: (b * nq + qi[p], hp)),
            pl.BlockSpec((MLA_TK, 2 * MLA_PAD), lambda b, hp, p, qi, kj: (b * nq + kj[p], hp)),
            pl.BlockSpec((MLA_TK, 2 * MLA_V), lambda b, hp, p, qi, kj: (b * nq + kj[p], hp)),
        ],
        out_specs=pl.BlockSpec((MLA_TQ, 2 * MLA_V), lambda b, hp, p, qi, kj: (b * nq + qi[p], hp)),
        scratch_shapes=[pltpu.VMEM((2, MLA_TQ, LANE), F32)] * 3,
    )
    return pl.pallas_call(
        _mla_kernel,
        out_shape=jax.ShapeDtypeStruct((T, MLA_WIDTH), F32),
        grid_spec=grid_spec,
        compiler_params=_cp(3),
        name="mla_flash",
    )(qi, kj, q, k, v)


def _outproj_kernel(x_ref, mod_ref, o1_ref, o2_ref, o3_ref, l1_ref, l2_ref, l3_ref, yb_ref, yc_ref,
                    woa_ref, wob_ref, woc_ref, mg_ref, n2g_ref, x1_o, h2_o, *, tiles_per_batch):
    b = pl.program_id(0) // tiles_per_batch
    g1 = mod_ref[pl.ds(b, 1), 2 * D_MODEL:3 * D_MODEL]
    sh2 = mod_ref[pl.ds(b, 1), 3 * D_MODEL:4 * D_MODEL]
    sc2 = mod_ref[pl.ds(b, 1), 4 * D_MODEL:5 * D_MODEL]
    l1, l2, l3 = l1_ref[...], l2_ref[...], l3_ref[...]
    mx = jnp.maximum(jnp.maximum(l1, l2), l3)
    w1, w2, w3 = jnp.exp(l1 - mx), jnp.exp(l2 - mx), jnp.exp(l3 - mx)
    ya = (w1 * o1_ref[...] + w2 * o2_ref[...] + w3 * o3_ref[...]) / (w1 + w2 + w3)
    ya = ya * lax.rsqrt(jnp.mean(ya * ya, axis=-1, keepdims=True) + EPS) * mg_ref[0:1, 0:SWA_WIDTH]
    yc = yc_ref[...]
    yc = yc * lax.rsqrt(jnp.mean(yc * yc, axis=-1, keepdims=True) + EPS) * mg_ref[1:2, 0:MLA_WIDTH]
    y = (_dot(ya.astype(BF16), woa_ref[...]) + _dot(yb_ref[...], wob_ref[...])
         + _dot(yc.astype(BF16), woc_ref[...]))
    x1 = x_ref[...] + g1 * y
    x1_o[...] = x1
    ms = jnp.mean(x1 * x1, axis=-1, keepdims=True)
    h2_o[...] = ((x1 * lax.rsqrt(ms + EPS) * n2g_ref[...]) * (1.0 + sc2) + sh2).astype(BF16)


def _outproj_call(x2d, mod_l, o_list, lse_list, yb, yc, lw, S):
    T = x2d.shape[0]
    tm = TM_PROJ
    row = lambda i: (i, 0)
    ins = [x2d, mod_l] + list(o_list) + list(lse_list) + [yb, yc, lw["woa"], lw["wob"], lw["woc"], lw["mg2"], lw["n2g"]]
    in_specs = ([pl.BlockSpec((tm, D_MODEL), row), _full(mod_l.shape)]
                + [pl.BlockSpec((tm, SWA_WIDTH), row)] * 6
                + [pl.BlockSpec((tm, S5_CHANNELS), row), pl.BlockSpec((tm, MLA_WIDTH), row)]
                + [_full(a.shape) for a in ins[10:]])
    return pl.pallas_call(
        functools.partial(_outproj_kernel, tiles_per_batch=S // tm),
        out_shape=[jax.ShapeDtypeStruct((T, D_MODEL), F32), jax.ShapeDtypeStruct((T, D_MODEL), BF16)],
        grid=(T // tm,),
        in_specs=in_specs,
        out_specs=[pl.BlockSpec((tm, D_MODEL), row)] * 2,
        compiler_params=_cp(1),
        name="out_proj",
    )(*ins)


def _row_select(rows, n):
    tm = rows[0].shape[1]
    rid = lax.broadcasted_iota(jnp.int32, (n, tm), 0)
    out = jnp.full((n, tm), NEG, F32)
    for i, r in enumerate(rows):
        out = jnp.where(rid == i, r, out)
    return out


def _top_rows(s, n):
    rows = []
    work = s
    for i in range(n):
        mx = jnp.max(work, axis=0, keepdims=True)
        rows.append(mx)
        if i + 1 < n:
            work = jnp.where(work >= mx, NEG, work)
    return rows


def _peer_score_kernel(h2_ref, wq_ref, sk_ref, th_o, e1z_o, s2_o, e2_o, q_sc):
    hd = pl.program_id(1)

    @pl.when(hd == 0)
    def _():
        q = _dot(h2_ref[...], wq_ref[...]).astype(BF16)
        for c in range(2 * PEER_HEADS):
            q_sc[c] = q[:, c * PEER_KEYS:(c + 1) * PEER_KEYS]

    s1 = _dot_nt(sk_ref[0, 0], q_sc[2 * hd])
    s2 = _dot_nt(sk_ref[0, 1], q_sc[2 * hd + 1])
    n_top = PEER_TOPK + 1
    n_pad = 3 * SUBLANE
    a_rows = _top_rows(s1, n_top)
    b_rows = _top_rows(s2, n_top)
    a_sorted = _row_select(a_rows, n_pad)
    b_sorted = _row_select(b_rows, n_pad)
    pieces = [a_rows[0] + b_sorted]
    for i in range(1, SUBLANE):
        pieces.append(a_rows[i] + b_sorted[0:SUBLANE])
    pieces.append(a_sorted[SUBLANE:n_pad] + b_rows[0])
    tops = []
    for kk in range(PEER_TOPK + 1):
        mx = functools.reduce(jnp.maximum, [jnp.max(pc, axis=0, keepdims=True) for pc in pieces])
        tops.append(mx)
        if kk < PEER_TOPK:
            pieces = [jnp.where(pc >= mx, NEG, pc) for pc in pieces]
    theta = 0.5 * (tops[PEER_TOPK - 1] + tops[PEER_TOPK])
    z = functools.reduce(lambda a, c: a + c, [jnp.exp(t - tops[0]) for t in tops[:PEER_TOPK]])
    th_o[0] = theta - s1
    e1z_o[0] = jnp.exp(s1 - a_rows[0]) / z
    s2_o[0] = s2
    e2_o[0] = jnp.exp(s2 - b_rows[0])


def _peer_score_call(h2, wq, sk):
    T = h2.shape[0]
    tm = PEER_TM
    out_spec = pl.BlockSpec((1, PEER_KEYS, tm), lambda i, h: (h, 0, i))
    return pl.pallas_call(
        _peer_score_kernel,
        out_shape=[jax.ShapeDtypeStruct((PEER_HEADS, PEER_KEYS, T), F32)] * 4,
        grid=(T // tm, PEER_HEADS),
        in_specs=[
            pl.BlockSpec((tm, D_MODEL), lambda i, h: (i, 0)),
            _full(wq.shape),
            pl.BlockSpec((1, 2, PEER_KEYS, PEER_DKEY // 2), lambda i, h: (h, 0, 0, 0)),
        ],
        out_specs=[out_spec] * 4,
        scratch_shapes=[pltpu.VMEM((2 * PEER_HEADS, tm, PEER_KEYS), BF16)],
        compiler_params=_cp(2),
        name="peer_scores",
    )(h2, wq, sk)


def _peer_dense_kernel(h2_ref, u_ref, vt_ref, th_ref, e1z_ref, s2_ref, e2_ref, x1_ref, mod_ref, o_ref,
                       act_sc, w_sc, acc_sc, *, tiles_per_batch, n_chunks):
    kc = pl.program_id(1)

    @pl.when(kc == 0)
    def _():
        acc_sc[...] = jnp.zeros_like(acc_sc)

    act_sc[...] = _dot_nt(u_ref[...], h2_ref[...])
    rows_per_chunk = PEER_EC // PEER_KEYS

    a0 = pl.multiple_of(kc * rows_per_chunk, rows_per_chunk)
    for lg in range(PEER_TM // LANE):
        ls = slice(lg * LANE, (lg + 1) * LANE)
        for al in range(rows_per_chunk):
            rs = slice(al * PEER_KEYS, (al + 1) * PEER_KEYS)
            gate = jnp.zeros((PEER_KEYS, LANE), F32)
            for h in range(PEER_HEADS):
                th = th_ref[h, pl.ds(a0, rows_per_chunk), ls][al:al + 1, :]
                e1 = e1z_ref[h, pl.ds(a0, rows_per_chunk), ls][al:al + 1, :]
                gate = gate + jnp.where(s2_ref[h, :, ls] >= th, e2_ref[h, :, ls] * e1, 0.0)
            w_sc[rs, ls] = (_gelu(act_sc[rs, ls]) * gate).astype(BF16)
    acc_sc[...] += _dot(vt_ref[...], w_sc[...])

    @pl.when(kc == n_chunks - 1)
    def _():
        b = pl.program_id(0) // tiles_per_batch
        g2 = mod_ref[pl.ds(b, 1), 5 * D_MODEL:6 * D_MODEL]
        o_ref[...] = x1_ref[...] + g2 * acc_sc[...].T


def _peer_dense_call(h2, u_b, vt_b, tabs, x1, mod_l, S):
    T = h2.shape[0]
    tm = PEER_TM
    nk = PEER_EXPERTS // PEER_EC
    tab_spec = pl.BlockSpec((PEER_HEADS, PEER_KEYS, tm), lambda i, k: (0, 0, i))
    return pl.pallas_call(
        functools.partial(_peer_dense_kernel, tiles_per_batch=S // tm, n_chunks=nk),
        out_shape=jax.ShapeDtypeStruct((T, D_MODEL), F32),
        grid=(T // tm, nk),
        in_specs=[
            pl.BlockSpec((tm, D_MODEL), lambda i, k: (i, 0)),
            pl.BlockSpec((PEER_EC, D_MODEL), lambda i, k: (k, 0)),
            pl.BlockSpec((D_MODEL, PEER_EC), lambda i, k: (0, k)),
            tab_spec, tab_spec, tab_spec, tab_spec,
            pl.BlockSpec((tm, D_MODEL), lambda i, k: (i, 0)),
            _full(mod_l.shape),
        ],
        out_specs=pl.BlockSpec((tm, D_MODEL), lambda i, k: (i, 0)),
        scratch_shapes=[pltpu.VMEM((PEER_EC, tm), F32), pltpu.VMEM((PEER_EC, tm), BF16),
                        pltpu.VMEM((D_MODEL, tm), F32)],
        compiler_params=_cp(2),
        name="peer_dense",
    )(h2, u_b, vt_b, *tabs, x1, mod_l)


def _t5_bucket(dist):
    max_exact = T5_BUCKETS // 2
    d_f = jnp.maximum(dist, 1).astype(F32)
    large = max_exact + (jnp.log(d_f / max_exact) / math.log(T5_MAX_DISTANCE / max_exact)
                         * (T5_BUCKETS - max_exact)).astype(jnp.int32)
    large = jnp.minimum(large, T5_BUCKETS - 1)
    return jnp.where(dist < max_exact, dist, large)


def _bucket_map(dil):
    qi = jnp.arange(SWA_BLOCK)[:, None]
    kj = jnp.arange(2 * SWA_BLOCK)[None, :]
    delta = SWA_BLOCK + qi - kj
    band = (delta >= 0) & (delta <= SWA_BLOCK)
    return jnp.where(band, _t5_bucket(dil * jnp.clip(delta, 0, SWA_BLOCK)), -1).astype(jnp.int32)


def _group_mean_matrix(width, groups):
    m = np.zeros((width, width), np.float32)
    for start, size in groups:
        m[start:start + size, start:start + size] = 1.0 / size
    return jnp.asarray(m, BF16)


def _rope_tables(S):
    pos = jnp.arange(S, dtype=F32)
    inv_freq = ROPE_THETA ** (-jnp.arange(0, MLA_ROPE, 2, dtype=F32) / MLA_ROPE)
    ang = pos[:, None] * inv_freq[None, :]
    cos, sin = jnp.cos(ang), jnp.sin(ang)
    one = jnp.ones((S, MLA_NOPE), F32)
    zero = jnp.zeros((S, MLA_PAD - MLA_QK), F32)
    cos_t = jnp.concatenate([one, cos, cos, zero], axis=1)
    sin_t = jnp.concatenate([0.0 * one, -sin, sin, zero], axis=1)
    return cos_t, sin_t


def _pad_heads(w, per_head_cols, place):
    lead = w.shape[:-1]
    wh = w.reshape(lead + (MLA_HEADS, per_head_cols))
    out = jnp.zeros(lead + (MLA_HEADS, MLA_PAD), w.dtype)
    for src, dst in place:
        out = out.at[..., dst[0]:dst[1]].set(wh[..., src[0]:src[1]])
    return out.reshape(lead + (MLA_HEADS * MLA_PAD,))


def _prep_layer(p, l):
    half = MLA_ROPE // 2
    w_in = p["w_in"][l]
    c0 = 3 * SWA_WIDTH
    c1 = c0 + S5_CHANNELS
    c2 = c1 + MLA_Q_LORA
    c3 = c2 + MLA_KV_LORA
    kr = w_in[:, c3:c3 + MLA_ROPE]
    zpad = lambda n: jnp.zeros((D_MODEL, n), F32)
    kr_a = jnp.concatenate([zpad(MLA_NOPE), kr, zpad(MLA_PAD - MLA_QK)], axis=1)
    kr_b = jnp.concatenate([zpad(MLA_NOPE), kr[:, half:], kr[:, :half], zpad(MLA_PAD - MLA_QK)], axis=1)
    lw = {
        "n1g": p["norm1_gain"][l][None, :],
        "n2g": p["norm2_gain"][l][None, :],
        "wa": w_in[:, 0:c0].astype(BF16),
        "wu": w_in[:, c0:c1].astype(BF16),
        "wc": jnp.concatenate([w_in[:, c1:c3], kr_a, kr_b], axis=1).astype(BF16),
    }
    w_uq = p["w_uq"][l]
    lw["wuqa"] = _pad_heads(w_uq, MLA_QK, [((0, MLA_QK), (0, MLA_QK))]).astype(BF16)
    lw["wuqb"] = _pad_heads(w_uq, MLA_QK, [((MLA_NOPE + half, MLA_QK), (MLA_NOPE, MLA_NOPE + half)),
                                           ((MLA_NOPE, MLA_NOPE + half), (MLA_NOPE + half, MLA_QK))]).astype(BF16)
    w_ukv = p["w_ukv"][l]
    lw["wkn"] = _pad_heads(w_ukv, MLA_NOPE + MLA_V, [((0, MLA_NOPE), (0, MLA_NOPE))]).astype(BF16)
    lw["wv"] = w_ukv.reshape(MLA_KV_LORA, MLA_HEADS, MLA_NOPE + MLA_V)[:, :, MLA_NOPE:].reshape(
        MLA_KV_LORA, MLA_WIDTH).astype(BF16)
    lw["g384"] = _group_mean_matrix(SWA_WIDTH, [(h * HEAD_DIM, HEAD_DIM) for h in range(SWA_HEADS)])
    lw["gq"] = _group_mean_matrix(MLA_HEADS * MLA_PAD,
                                  [(h * MLA_PAD, MLA_NOPE) for h in range(MLA_HEADS)]
                                  + [(h * MLA_PAD + MLA_NOPE, MLA_ROPE) for h in range(MLA_HEADS)])
    lw["gk"] = _group_mean_matrix(MLA_HEADS * MLA_PAD, [(h * MLA_PAD, MLA_NOPE) for h in range(MLA_HEADS)])
    lw["swag"] = jnp.stack([jnp.tile(p["swa_q_gain"][l], SWA_HEADS) * (HEAD_DIM ** -0.5),
                            jnp.tile(p["swa_k_gain"][l], SWA_HEADS)])
    lw["lorag"] = jnp.stack([p["mla_q_lora_gain"][l],
                             jnp.concatenate([p["mla_kv_lora_gain"][l], jnp.zeros((MLA_Q_LORA - MLA_KV_LORA,), F32)])])
    qg = p["mla_q_gain"][l] * (MLA_QK ** -0.5)
    kg = p["mla_k_gain"][l]
    z32 = jnp.zeros((MLA_PAD - MLA_QK,), F32)
    z64 = jnp.zeros((MLA_NOPE,), F32)
    swap = lambda g: jnp.concatenate([g[MLA_NOPE + half:], g[MLA_NOPE:MLA_NOPE + half]])
    lw["qg"] = jnp.stack([jnp.tile(jnp.concatenate([qg, z32]), MLA_HEADS),
                          jnp.tile(jnp.concatenate([z64, swap(qg), z32]), MLA_HEADS)])
    lw["kg"] = jnp.stack([jnp.tile(jnp.concatenate([kg[:MLA_NOPE], z64]), MLA_HEADS),
                          jnp.tile(jnp.concatenate([z64, kg[MLA_NOPE:], z32]), MLA_HEADS),
                          jnp.tile(jnp.concatenate([z64, swap(kg), z32]), MLA_HEADS)])

    a_re, a_im = p["s5_a_re"][l], p["s5_a_im"][l]
    dt = jnp.exp(p["s5_log_dt"][l])[:, None]
    mag = jnp.exp(a_re * dt)
    lam_re, lam_im = mag * jnp.cos(a_im * dt), mag * jnp.sin(a_im * dt)
    den = a_re * a_re + a_im * a_im
    nr, ni = lam_re - 1.0, lam_im
    coef_re = (nr * a_re + ni * a_im) / den
    coef_im = (ni * a_re - nr * a_im) / den
    b_re, b_im = p["s5_b_re"][l], p["s5_b_im"][l]
    bb_re = coef_re[..., None] * b_re - coef_im[..., None] * b_im
    bb_im = coef_re[..., None] * b_im + coef_im[..., None] * b_re
    eye = jnp.eye(S5_GROUPS, dtype=F32)

    def in_blockdiag(bb):
        return jnp.einsum("gpc,gh->gchp", bb, eye).reshape(S5_CHANNELS, S5_LANES).astype(BF16)

    def out_blockdiag(cc):
        return jnp.einsum("gcp,gh->gphc", cc, eye).reshape(S5_LANES, S5_CHANNELS).astype(BF16)

    lw["bbre"], lw["bbim"] = in_blockdiag(bb_re), in_blockdiag(bb_im)
    lw["cre"], lw["cim"] = out_blockdiag(p["s5_c_re"][l]), out_blockdiag(p["s5_c_im"][l])

    def powers(ks, rows):
        kk = jnp.asarray(ks, F32)[:, None, None]
        m = jnp.exp(kk * (a_re * dt)[None])
        ang = kk * (a_im * dt)[None]
        re = (m * jnp.cos(ang)).reshape(len(ks), S5_LANES)
        im = (m * jnp.sin(ang)).reshape(len(ks), S5_LANES)
        reps = rows // len(ks)
        return jnp.stack([jnp.tile(re, (reps, 1)), jnp.tile(im, (reps, 1))])

    lw["pneg"] = powers([-t for t in range(S5_INNER)], S5_CHUNK)
    lw["ppos"] = powers(list(range(S5_INNER)), S5_CHUNK)
    lw["pcar"] = powers([t + 1 for t in range(S5_INNER)], S5_INNER)
    t_idx = np.arange(S5_CHUNK)
    tri = (t_idx[:, None] // S5_INNER == t_idx[None, :] // S5_INNER) & (t_idx[None, :] <= t_idx[:, None])
    lw["tri"] = jnp.asarray(tri.astype(np.float32), BF16)
    lw["dsk"] = p["s5_d"][l][None, :]
    lw["wglu"] = p["w_glu"][l].astype(BF16)
    lw["bglu"] = p["b_glu"][l][None, :]
    mg = p["mix_gain"][l]
    lw["mgb"] = mg[SWA_WIDTH:SWA_WIDTH + S5_CHANNELS][None, :]
    lw["mg2"] = jnp.stack([mg[:SWA_WIDTH], mg[SWA_WIDTH + S5_CHANNELS:]])
    w_out = p["w_out"][l].astype(BF16)
    lw["woa"] = w_out[:SWA_WIDTH]
    lw["wob"] = w_out[SWA_WIDTH:SWA_WIDTH + S5_CHANNELS]
    lw["woc"] = w_out[SWA_WIDTH + S5_CHANNELS:]
    lw["wq"] = p["peer_wq"][l].astype(BF16)
    lw["sk"] = p["peer_subkeys"][l].astype(BF16)
    lw["u"] = p["peer_u"][l].astype(BF16)
    lw["vt"] = p["peer_v"][l].T.astype(BF16)
    return lw


def _layer(x2d, mod_l, lw, rel_bias, idx_maps, cos_t, sin_t, B, S):
    qa, ka, va, u, q, k, v = _inproj_call(x2d, mod_l, lw, cos_t, sin_t, S)
    o_list, lse_list = [], []
    for (_, dil), idx_map in zip(DILATED_BRANCHES, idx_maps):
        o, lse = _swa_branch(qa, ka, va, rel_bias, idx_map, B, S, dil)
        o_list.append(o)
        lse_list.append(lse)
    yb = _s5_call(u, lw, B, S)
    yc = _mla_call(q, k, v, B, S)
    x1, h2 = _outproj_call(x2d, mod_l, o_list, lse_list, yb, yc, lw, S)
    tabs = _peer_score_call(h2, lw["wq"], lw["sk"])
    return _peer_dense_call(h2, lw["u"], lw["vt"], tabs, x1, mod_l, S)


def kernel(x, c, w_in, w_out, mix_gain, norm1_gain, norm2_gain, w_ada, b_ada, rel_bias, swa_q_gain, swa_k_gain, mla_q_lora_gain, mla_kv_lora_gain, w_uq, w_ukv, mla_q_gain, mla_k_gain, s5_a_re, s5_a_im, s5_b_re, s5_b_im, s5_c_re, s5_c_im, s5_d, s5_log_dt, w_glu, b_glu, peer_wq, peer_subkeys, peer_u, peer_v):
    B, S, D = x.shape
    L = w_in.shape[0]
    assert D == D_MODEL and S % (DILATED_BRANCHES[-1][1] * SWA_BLOCK) == 0 and S % MLA_TQ == 0
    p = dict(w_in=w_in, w_out=w_out, mix_gain=mix_gain, norm1_gain=norm1_gain, norm2_gain=norm2_gain,
             swa_q_gain=swa_q_gain, swa_k_gain=swa_k_gain, mla_q_lora_gain=mla_q_lora_gain,
             mla_kv_lora_gain=mla_kv_lora_gain, w_uq=w_uq, w_ukv=w_ukv, mla_q_gain=mla_q_gain,
             mla_k_gain=mla_k_gain, s5_a_re=s5_a_re, s5_a_im=s5_a_im, s5_b_re=s5_b_re, s5_b_im=s5_b_im,
             s5_c_re=s5_c_re, s5_c_im=s5_c_im, s5_d=s5_d, s5_log_dt=s5_log_dt, w_glu=w_glu, b_glu=b_glu,
             peer_wq=peer_wq, peer_subkeys=peer_subkeys, peer_u=peer_u, peer_v=peer_v)
    mod = _mod_call(c, w_ada, b_ada)
    cos_t, sin_t = _rope_tables(S)
    idx_maps = [_bucket_map(dil) for _, dil in DILATED_BRANCHES]
    x2d = x.reshape(B * S, D)
    for l in range(L):
        x2d = _layer(x2d, mod[l], _prep_layer(p, l), rel_bias, idx_maps, cos_t, sin_t, B, S)
    return x2d.reshape(B, S, D)
```
